```python
import jax, jax.numpy as jnp
from jax import lax
import numpy as np

D_MODEL = 4096
BATCH = 4
SEQ = 4096
DEPTH = 4

MIX_W = D_MODEL
ATT_W = MIX_W // 2
HEAD_DIM = 128
N_Q_HEADS = ATT_W // HEAD_DIM
N_KV_HEADS = N_Q_HEADS // 4
KV_W = N_KV_HEADS * HEAD_DIM
AXIS_DIM = HEAD_DIM // 2
ROPE_THETA = 10000.0
Q_BLOCK = 128
GRID_W = 64

REC_W = MIX_W - ATT_W
REC_HEADS = 16
REC_BLK = REC_W // REC_HEADS
REC_CONV = 4
RG_C = 8.0

IN_W = ATT_W + 2 * KV_W + 2 * REC_W

N_MEM = 256
X_HEADS = 4
X_HEAD_DIM = 256
X_W = X_HEADS * X_HEAD_DIM

D_FF = 3 * D_MODEL
FFN_CONV = 3
EPS = 1e-6

kernel_name = "bidir_hymba_rglru_gqa_axialrope_convffn"


def rmsnorm(x, g):
    xf = x.astype(jnp.float32)
    y = xf * lax.rsqrt(jnp.mean(xf * xf, axis=-1, keepdims=True) + EPS)
    return (y * g.astype(jnp.float32)).astype(x.dtype)


def dwconv_centred(x, w, b):
    K = w.shape[0]
    left = (K - 1) // 2
    right = K - 1 - left
    S = x.shape[1]
    xp = jnp.pad(x, ((0, 0), (left, right), (0, 0)))
    y = xp[:, 0:S] * w[0]
    for k in range(1, K):
        y = y + xp[:, k:k + S] * w[k]
    return y + b


def axial_rope_tables(S):
    rows = S // GRID_W
    row = jnp.repeat(jnp.arange(rows, dtype=jnp.float32), GRID_W)
    col = jnp.tile(jnp.arange(GRID_W, dtype=jnp.float32), rows)
    inv = ROPE_THETA ** (-jnp.arange(0, AXIS_DIM, 2, dtype=jnp.float32) / AXIS_DIM)
    ang = jnp.stack([row[:, None] * inv, col[:, None] * inv], axis=1)
    return jnp.cos(ang), jnp.sin(ang)


def apply_axial_rope(x, cos, sin):
    B, S, H, hd = x.shape
    xf = x.astype(jnp.float32).reshape(B, S, H, 2, 2, AXIS_DIM // 2)
    x1 = xf[..., 0, :]
    x2 = xf[..., 1, :]
    c = cos[None, :, None]
    s = sin[None, :, None]
    out = jnp.stack([x1 * c - x2 * s, x2 * c + x1 * s], axis=-2)
    return out.reshape(B, S, H, hd).astype(x.dtype)


def blocked_gqa(q, k, v):
    B, S, H, hd = q.shape
    KV = k.shape[2]
    G = H // KV
    NB = S // Q_BLOCK
    qb = q.reshape(B, NB, Q_BLOCK, KV, G, hd).transpose(1, 0, 2, 3, 4, 5)
    scale = hd ** -0.5

    def one_block(qblk):
        s = jnp.einsum('bqkgd,bskd->bkgqs', qblk, k,
                       preferred_element_type=jnp.float32) * scale
        p = jax.nn.softmax(s, axis=-1).astype(v.dtype)
        return jnp.einsum('bkgqs,bskd->bqkgd', p, v)

    o = lax.map(one_block, qb)
    return o.transpose(1, 0, 2, 3, 4, 5).reshape(B, S, H * hd)


def rglru_direction(xc, w_gates, b_gates, lam, reverse):
    B, S, W = xc.shape
    xh = xc.reshape(B, S, REC_HEADS, REC_BLK)
    gates = jnp.einsum('bshi,ghij->gbshj', xh, w_gates).reshape(2, B, S, W) + b_gates[:, None, None, :]
    r = jax.nn.sigmoid(gates[0].astype(jnp.float32))
    i = jax.nn.sigmoid(gates[1].astype(jnp.float32))
    log_a = -RG_C * r * jax.nn.softplus(-lam.astype(jnp.float32))
    a = jnp.exp(log_a)
    b = jnp.sqrt(-jnp.expm1(2.0 * log_a)) * (i * xc.astype(jnp.float32))
    if reverse:
        a = jnp.flip(a, axis=1)
        b = jnp.flip(b, axis=1)

    def combine(lhs, rhs):
        a1, b1 = lhs
        a2, b2 = rhs
        return a1 * a2, a2 * b1 + b2

    _, h = lax.associative_scan(combine, (a, b), axis=1)
    if reverse:
        h = jnp.flip(h, axis=1)
    return h


def hybrid_mixer(h, w_in, q_g, k_g, conv_w, conv_b, gate_w, gate_b, lam, group_g, w_out, cos, sin):
    B, S, _ = h.shape
    z = h @ w_in
    q, k, v, xr, gr = jnp.split(
        z, [ATT_W, ATT_W + KV_W, ATT_W + 2 * KV_W, ATT_W + 2 * KV_W + REC_W], axis=-1)
    q = q.reshape(B, S, N_Q_HEADS, HEAD_DIM)
    k = k.reshape(B, S, N_KV_HEADS, HEAD_DIM)
    v = v.reshape(B, S, N_KV_HEADS, HEAD_DIM)
    q = apply_axial_rope(rmsnorm(q, q_g), cos, sin)
    k = apply_axial_rope(rmsnorm(k, k_g), cos, sin)
    att = blocked_gqa(q, k, v)
    xc = dwconv_centred(xr, conv_w, conv_b)
    hs = (rglru_direction(xc, gate_w[0], gate_b[0], lam[0], False)
          + rglru_direction(xc, gate_w[1], gate_b[1], lam[1], True))
    rec = (jax.nn.gelu(gr.astype(jnp.float32), approximate=True) * hs).astype(h.dtype)
    y = jnp.concatenate([rmsnorm(att, group_g[:ATT_W]), rmsnorm(rec, group_g[ATT_W:])], axis=-1)
    return y @ w_out


def memory_cross_attn(h, mem_n, w_cq, w_ckv, w_co):
    B, S, _ = h.shape
    M = mem_n.shape[1]
    q = (h @ w_cq).reshape(B, S, X_HEADS, X_HEAD_DIM)
    kv = (mem_n @ w_ckv).reshape(B, M, 2, X_HEADS, X_HEAD_DIM)
    k = kv[:, :, 0]
    v = kv[:, :, 1]
    s = jnp.einsum('bqhd,bmhd->bhqm', q, k, preferred_element_type=jnp.float32) * (X_HEAD_DIM ** -0.5)
    p = jax.nn.softmax(s, axis=-1).astype(v.dtype)
    o = jnp.einsum('bhqm,bmhd->bqhd', p, v).reshape(B, S, X_W)
    return o @ w_co


def conv_gated_mlp(h, w_up, conv_w, conv_b, w_down):
    u = dwconv_centred(h @ w_up, conv_w, conv_b)
    g, val = jnp.split(u, 2, axis=-1)
    return (jax.nn.gelu(g, approximate=True) * val) @ w_down


def _dense(k, shape, fan_in):
    return jax.random.normal(k, shape, jnp.float32) * (fan_in ** -0.5)


def _gain(k, shape):
    return 1.0 + 0.02 * jax.random.normal(k, shape, jnp.float32)


def _bias(k, shape):
    return 0.02 * jax.random.normal(k, shape, jnp.float32)


def setup_inputs(seed: int = 0) -> dict:
    key = jax.random.key(seed)
    ks = jax.random.split(key, 24)
    x = jax.random.normal(ks[0], (BATCH, SEQ, D_MODEL), jnp.float32)
    mem = jax.random.normal(ks[1], (BATCH, N_MEM, D_MODEL), jnp.float32)
    u = jax.random.uniform(ks[11], (DEPTH, 2, REC_W), jnp.float32, 0.81, 0.998)
    a = jnp.sqrt(u) ** (1.0 / RG_C)
    rg_lambda = jnp.log(a) - jnp.log1p(-a)
    return {
        "x": x,
        "mem": mem,
        "mem_norm": _gain(ks[2], (D_MODEL,)),
        "norm_mix": _gain(ks[3], (DEPTH, D_MODEL)),
        "w_in": _dense(ks[4], (DEPTH, D_MODEL, IN_W), D_MODEL),
        "q_norm": _gain(ks[5], (DEPTH, HEAD_DIM)),
        "k_norm": _gain(ks[6], (DEPTH, HEAD_DIM)),
        "rg_conv_w": _dense(ks[7], (DEPTH, REC_CONV, REC_W), REC_CONV),
        "rg_conv_b": _bias(ks[8], (DEPTH, REC_W)),
        "rg_gate_w": _dense(ks[9], (DEPTH, 2, 2, REC_HEADS, REC_BLK, REC_BLK), REC_BLK),
        "rg_gate_b": _bias(ks[10], (DEPTH, 2, 2, REC_W)),
        "rg_lambda": rg_lambda,
        "group_norm": _gain(ks[12], (DEPTH, MIX_W)),
        "w_out": _dense(ks[13], (DEPTH, MIX_W, D_MODEL), MIX_W),
        "norm_cross": _gain(ks[14], (DEPTH, D_MODEL)),
        "w_cq": _dense(ks[15], (DEPTH, D_MODEL, X_W), D_MODEL),
        "w_ckv": _dense(ks[16], (DEPTH, D_MODEL, 2 * X_W), D_MODEL),
        "w_co": _dense(ks[17], (DEPTH, X_W, D_MODEL), X_W),
        "norm_ffn": _gain(ks[18], (DEPTH, D_MODEL)),
        "w_up": _dense(ks[19], (DEPTH, D_MODEL, 2 * D_FF), D_MODEL),
        "ffn_conv_w": _dense(ks[20], (DEPTH, FFN_CONV, 2 * D_FF), FFN_CONV),
        "ffn_conv_b": _bias(ks[21], (DEPTH, 2 * D_FF)),
        "w_down": _dense(ks[22], (DEPTH, D_FF, D_MODEL), D_FF),
        "final_norm": _gain(ks[23], (D_MODEL,)),
    }


def reference(x, mem, mem_norm, norm_mix, w_in, q_norm, k_norm, rg_conv_w, rg_conv_b,
              rg_gate_w, rg_gate_b, rg_lambda, group_norm, w_out, norm_cross, w_cq, w_ckv,
              w_co, norm_ffn, w_up, ffn_conv_w, ffn_conv_b, w_down, final_norm):
    S = x.shape[1]
    cos, sin = axial_rope_tables(S)
    mem_n = rmsnorm(mem, mem_norm)
    h = x
    for l in range(DEPTH):
        h = h + hybrid_mixer(rmsnorm(h, norm_mix[l]), w_in[l], q_norm[l], k_norm[l],
                             rg_conv_w[l], rg_conv_b[l], rg_gate_w[l], rg_gate_b[l],
                             rg_lambda[l], group_norm[l], w_out[l], cos, sin)
        h = h + memory_cross_attn(rmsnorm(h, norm_cross[l]), mem_n, w_cq[l], w_ckv[l], w_co[l])
        h = h + conv_gated_mlp(rmsnorm(h, norm_ffn[l]), w_up[l], ffn_conv_w[l], ffn_conv_b[l], w_down[l])
    return rmsnorm(h, final_norm)
```

```python
import functools
import math

import jax
import jax.numpy as jnp
from jax import lax
from jax.experimental import pallas as pl
from jax.experimental.pallas import tpu as pltpu

_F32 = jnp.float32
_BF16 = jnp.bfloat16

HEAD_DIM = 128
GQA_GROUP = 4
GRID_W = 64
ROPE_THETA = 10000.0
RG_C = 8.0
X_HEADS = 4
EPS = 1e-6

F32_SUBLANES = 8
BF16_SUBLANES = 16
VMEM_LIMIT_BYTES = 56 * 2**20


def _tile(dim, pref):
    t = min(pref, dim)
    while dim % t:
        t //= 2
    return t


def _params(n_axes, vmem_bytes=VMEM_LIMIT_BYTES):
    return pltpu.CompilerParams(dimension_semantics=("arbitrary",) * n_axes,
                                vmem_limit_bytes=vmem_bytes)


def _rmsnorm_body(x_ref, g_ref, o_ref):
    x = x_ref[...].astype(_F32)
    inv = lax.rsqrt(jnp.mean(x * x, axis=-1, keepdims=True) + EPS)
    o_ref[...] = ((x * inv) * g_ref[...]).astype(o_ref.dtype)


def _rmsnorm(x, g, out_dtype):
    m, d = x.shape
    tm = _tile(m, 256)
    return pl.pallas_call(
        _rmsnorm_body,
        grid=(m // tm,),
        in_specs=[pl.BlockSpec((tm, d), lambda i: (i, 0)),
                  pl.BlockSpec((1, d), lambda i: (0, 0))],
        out_specs=pl.BlockSpec((tm, d), lambda i: (i, 0)),
        out_shape=jax.ShapeDtypeStruct((m, d), out_dtype),
        compiler_params=_params(1),
        name="rmsnorm",
    )(x, g.reshape(1, d).astype(_F32))


def _matmul_body(*refs, nk, has_res):
    if has_res:
        x_ref, w_ref, r_ref, o_ref = refs[:4]
        rest = refs[4:]
    else:
        x_ref, w_ref, o_ref = refs[:3]
        r_ref = None
        rest = refs[3:]
    part = jnp.dot(x_ref[...], w_ref[...], preferred_element_type=_F32)
    if nk == 1:
        if has_res:
            part = part + r_ref[...]
        o_ref[...] = part.astype(o_ref.dtype)
        return
    acc_ref = rest[0]
    k = pl.program_id(2)

    @pl.when(k == 0)
    def _():
        acc_ref[...] = part

    @pl.when(k > 0)
    def _():
        acc_ref[...] += part

    @pl.when(k == nk - 1)
    def _():
        acc = acc_ref[...]
        if has_res:
            acc = acc + r_ref[...]
        o_ref[...] = acc.astype(o_ref.dtype)


def _matmul(x, w, res=None, out_dtype=_F32, tm=512, tn=1024, tk=4096):
    m, kd = x.shape
    n = w.shape[1]
    tm, tn, tk = _tile(m, tm), _tile(n, tn), _tile(kd, tk)
    nk = kd // tk
    has_res = res is not None
    in_specs = [pl.BlockSpec((tm, tk), lambda i, j, k: (i, k)),
                pl.BlockSpec((tk, tn), lambda i, j, k: (k, j))]
    args = [x, w]
    if has_res:
        in_specs.append(pl.BlockSpec((tm, tn), lambda i, j, k: (i, j)))
        args.append(res)
    return pl.pallas_call(
        functools.partial(_matmul_body, nk=nk, has_res=has_res),
        grid=(m // tm, n // tn, nk),
        in_specs=in_specs,
        out_specs=pl.BlockSpec((tm, tn), lambda i, j, k: (i, j)),
        out_shape=jax.ShapeDtypeStruct((m, n), out_dtype),
        scratch_shapes=[pltpu.VMEM((tm, tn), _F32)] if nk > 1 else [],
        compiler_params=_params(3),
        name="matmul",
    )(*args)


def _swap_half_pairs(y, lane):
    upper = (lane & (HEAD_DIM // 4)) != 0
    return jnp.where(upper, pltpu.roll(y, HEAD_DIM // 4, 1), pltpu.roll(y, 3 * HEAD_DIM // 4, 1))


def _qkv_prep_body(zq_ref, zk_ref, zv_ref, cos_ref, sin_ref, qg_ref, kg_ref,
                   q_ref, k_ref, v_ref, *, n_q, n_kv):
    cos = cos_ref[...]
    sin = sin_ref[...]
    lane = lax.broadcasted_iota(jnp.int32, cos.shape, 1)

    def norm_rope(x, g):
        y = (x * lax.rsqrt(jnp.mean(x * x, axis=-1, keepdims=True) + EPS)) * g
        return y * cos + _swap_half_pairs(y, lane) * sin

    scale = HEAD_DIM ** -0.5
    for h in range(n_q):
        sl = slice(h * HEAD_DIM, (h + 1) * HEAD_DIM)
        q_ref[:, sl] = (norm_rope(zq_ref[:, sl], qg_ref[...]) * scale).astype(_BF16)
    for h in range(n_kv):
        sl = slice(h * HEAD_DIM, (h + 1) * HEAD_DIM)
        k_ref[:, sl] = norm_rope(zk_ref[:, sl], kg_ref[...]).astype(_BF16)
    v_ref[...] = zv_ref[...].astype(_BF16)


def _qkv_prep(z, cos_t, sin_t, qg, kg, att_w, kv_w, seq):
    t = z.shape[0]
    tm = _tile(seq, 256)
    n_seq = seq // tm
    kv_blk = att_w // kv_w
    return pl.pallas_call(
        functools.partial(_qkv_prep_body, n_q=att_w // HEAD_DIM, n_kv=kv_w // HEAD_DIM),
        grid=(t // tm,),
        in_specs=[pl.BlockSpec((tm, att_w), lambda i: (i, 0)),
                  pl.BlockSpec((tm, kv_w), lambda i: (i, kv_blk)),
                  pl.BlockSpec((tm, kv_w), lambda i: (i, kv_blk + 1)),
                  pl.BlockSpec((tm, HEAD_DIM), lambda i: (i % n_seq, 0)),
                  pl.BlockSpec((tm, HEAD_DIM), lambda i: (i % n_seq, 0)),
                  pl.BlockSpec((1, HEAD_DIM), lambda i: (0, 0)),
                  pl.BlockSpec((1, HEAD_DIM), lambda i: (0, 0))],
        out_specs=[pl.BlockSpec((tm, att_w), lambda i: (i, 0)),
                   pl.BlockSpec((tm, kv_w), lambda i: (i, 0)),
                   pl.BlockSpec((tm, kv_w), lambda i: (i, 0))],
        out_shape=[jax.ShapeDtypeStruct((t, att_w), _BF16),
                   jax.ShapeDtypeStruct((t, kv_w), _BF16),
                   jax.ShapeDtypeStruct((t, kv_w), _BF16)],
        compiler_params=_params(1),
        name="qkv_prep",
    )(z, z, z, cos_t, sin_t, qg.reshape(1, HEAD_DIM), kg.reshape(1, HEAD_DIM))


def _attn_body(q_ref, k_ref, v_ref, o_ref):
    s = lax.dot_general(q_ref[...], k_ref[...], (((1,), (1,)), ((), ())),
                        preferred_element_type=_F32)
    p = jnp.exp(s - jnp.max(s, axis=-1, keepdims=True))
    l = jnp.sum(p, axis=-1, keepdims=True)
    o = jnp.dot(p.astype(_BF16), v_ref[...], preferred_element_type=_F32)
    o_ref[...] = o / l


def _attention(q, k, v, batch, seq):
    t, att_w = q.shape
    n_kv = k.shape[1] // HEAD_DIM
    tq = _tile(seq, 512)
    nq = seq // tq
    return pl.pallas_call(
        _attn_body,
        grid=(batch, n_kv, nq, GQA_GROUP),
        in_specs=[pl.BlockSpec((tq, HEAD_DIM), lambda b, g, i, h: (b * nq + i, g * GQA_GROUP + h)),
                  pl.BlockSpec((seq, HEAD_DIM), lambda b, g, i, h: (b, g)),
                  pl.BlockSpec((seq, HEAD_DIM), lambda b, g, i, h: (b, g))],
        out_specs=pl.BlockSpec((tq, HEAD_DIM), lambda b, g, i, h: (b * nq + i, g * GQA_GROUP + h)),
        out_shape=jax.ShapeDtypeStruct((t, att_w), _F32),
        compiler_params=_params(4),
        name="self_attention",
    )(q, k, v)


def _rec_conv_body(xp_ref, x_ref, xn_ref, w_ref, b_ref, o_ref, ext_ref, *, tm, seq_tiles):
    i = pl.program_id(0)
    first = (i % seq_tiles) == 0
    last = (i % seq_tiles) == seq_tiles - 1
    hs = F32_SUBLANES
    ext_ref[0:hs] = jnp.where(first, 0.0, xp_ref[0])
    ext_ref[hs:hs + tm] = x_ref[...]
    ext_ref[hs + tm:hs + tm + hs] = jnp.where(last, 0.0, xn_ref[0])
    w = w_ref[...]
    y = ext_ref[hs - 1:hs - 1 + tm] * w[0:1]
    y = y + ext_ref[hs:hs + tm] * w[1:2]
    y = y + ext_ref[hs + 1:hs + 1 + tm] * w[2:3]
    y = y + ext_ref[hs + 2:hs + 2 + tm] * w[3:4]
    o_ref[...] = y + b_ref[...]


def _rec_conv(z, conv_w, conv_b, rec_w, col0, seq):
    t, zw = z.shape
    tm = _tile(seq, 512)
    tc = _tile(math.gcd(rec_w, col0), 1024)
    c0 = col0 // tc
    hs = F32_SUBLANES
    z3 = z.reshape(t // hs, hs, zw)
    nb = t // hs
    return pl.pallas_call(
        functools.partial(_rec_conv_body, tm=tm, seq_tiles=seq // tm),
        grid=(t // tm, rec_w // tc),
        in_specs=[pl.BlockSpec((1, hs, tc), lambda i, c: (jnp.maximum(i * (tm // hs) - 1, 0), 0, c0 + c)),
                  pl.BlockSpec((tm, tc), lambda i, c: (i, c0 + c)),
                  pl.BlockSpec((1, hs, tc), lambda i, c: (jnp.minimum((i + 1) * (tm // hs), nb - 1), 0, c0 + c)),
                  pl.BlockSpec((4, tc), lambda i, c: (0, c)),
                  pl.BlockSpec((1, tc), lambda i, c: (0, c))],
        out_specs=pl.BlockSpec((tm, tc), lambda i, c: (i, c)),
        out_shape=jax.ShapeDtypeStruct((t, rec_w), _F32),
        scratch_shapes=[pltpu.VMEM((tm + 2 * hs, tc), _F32)],
        compiler_params=_params(2),
        name="rec_conv",
    )(z3, z, z3, conv_w, conv_b.reshape(1, rec_w))


def _rglru_body(xc_ref, w_ref, gb_ref, lam_ref, h_ref, a_ref, b_ref, carry_ref,
                *, reverse, tt, heads, blk):
    j = pl.program_id(2)
    tc = heads * blk
    hs = F32_SUBLANES

    @pl.when(j == 0)
    def _():
        carry_ref[...] = jnp.zeros_like(carry_ref)

    neg_lam = -lam_ref[...]
    softplus = jnp.maximum(neg_lam, 0.0) + jnp.log1p(jnp.exp(-jnp.abs(neg_lam)))
    for hd in range(heads):
        sl = slice(hd * blk, (hd + 1) * blk)
        xh = xc_ref[:, sl]
        xb = xh.astype(_BF16)
        g_r = jnp.dot(xb, w_ref[0, hd], preferred_element_type=_F32) + gb_ref[0:1, sl]
        g_i = jnp.dot(xb, w_ref[1, hd], preferred_element_type=_F32) + gb_ref[1:2, sl]
        log_a = (-RG_C) * jax.nn.sigmoid(g_r) * softplus[:, sl]
        a = jnp.exp(log_a)
        a_ref[:, sl] = a
        one_minus_a2 = jnp.tanh(-log_a) * (1.0 + a * a)
        b_ref[:, sl] = jnp.sqrt(one_minus_a2) * (jax.nn.sigmoid(g_i) * xh)

    ng = tt // hs
    row = lax.broadcasted_iota(jnp.int32, (hs, tc), 0)

    def step(g, carry):
        gg = (ng - 1 - g) if reverse else g
        off = pl.multiple_of(gg * hs, hs)
        a = a_ref[pl.ds(off, hs), :]
        b = b_ref[pl.ds(off, hs), :]
        for k in (1, 2, 4):
            shift = hs - k if reverse else k
            mask = (row < hs - k) if reverse else (row >= k)
            b = jnp.where(mask, a * pltpu.roll(b, shift, 0) + b, b)
            a = jnp.where(mask, a * pltpu.roll(a, shift, 0), a)
        h = a * carry + b
        h_ref[pl.ds(off, hs), :] = h
        edge = h[0:1] if reverse else h[hs - 1:hs]
        return jnp.broadcast_to(edge, (hs, tc))

    carry_ref[...] = lax.fori_loop(0, ng, step, carry_ref[...])


def _rglru_scan(xc, gate_w, gate_b, lam, batch, seq, reverse):
    t, rec_w = xc.shape
    n_heads, blk = gate_w.shape[1], gate_w.shape[2]
    tt = _tile(seq, 512)
    nt = seq // tt
    heads = max(1, _tile(rec_w, 1024) // blk)
    tc = heads * blk

    def tmap(b, c, j):
        jj = (nt - 1 - j) if reverse else j
        return (b * nt + jj, c)

    return pl.pallas_call(
        functools.partial(_rglru_body, reverse=reverse, tt=tt, heads=heads, blk=blk),
        grid=(batch, rec_w // tc, nt),
        in_specs=[pl.BlockSpec((tt, tc), tmap),
                  pl.BlockSpec((2, heads, blk, blk), lambda b, c, j: (0, c, 0, 0)),
                  pl.BlockSpec((2, tc), lambda b, c, j: (0, c)),
                  pl.BlockSpec((1, tc), lambda b, c, j: (0, c))],
        out_specs=pl.BlockSpec((tt, tc), tmap),
        out_shape=jax.ShapeDtypeStruct((t, rec_w), _F32),
        scratch_shapes=[pltpu.VMEM((tt, tc), _F32), pltpu.VMEM((tt, tc), _F32),
                        pltpu.VMEM((F32_SUBLANES, tc), _F32)],
        compiler_params=_params(3),
        name="rglru_rev" if reverse else "rglru_fwd",
    )(xc, gate_w.astype(_BF16), gate_b, lam.reshape(1, rec_w))


def _mix_post_body(att_ref, hf_ref, hb_ref, g1_ref, g2_ref, gn_ref, o_ref, *, att_w, half):
    att = att_ref[...]
    inv = lax.rsqrt(jnp.mean(att * att, axis=-1, keepdims=True) + EPS)
    o_ref[:, :att_w] = ((att * inv) * gn_ref[:, :att_w]).astype(o_ref.dtype)
    hs = hf_ref[...] + hb_ref[...]
    r1 = jax.nn.gelu(g1_ref[...], approximate=True) * hs[:, :half]
    r2 = jax.nn.gelu(g2_ref[...], approximate=True) * hs[:, half:]
    ssq = jnp.sum(r1 * r1, axis=-1, keepdims=True) + jnp.sum(r2 * r2, axis=-1, keepdims=True)
    inv = lax.rsqrt(ssq / (2 * half) + EPS)
    o_ref[:, att_w:att_w + half] = ((r1 * inv) * gn_ref[:, att_w:att_w + half]).astype(o_ref.dtype)
    o_ref[:, att_w + half:] = ((r2 * inv) * gn_ref[:, att_w + half:]).astype(o_ref.dtype)


def _mix_post(att, hf, hb, z, group_g, gate_col0):
    t, att_w = att.shape
    rec_w = hf.shape[1]
    half = rec_w // 2
    gc = gate_col0 // half
    tm = _tile(t, 256)
    return pl.pallas_call(
        functools.partial(_mix_post_body, att_w=att_w, half=half),
        grid=(t // tm,),
        in_specs=[pl.BlockSpec((tm, att_w), lambda i: (i, 0)),
                  pl.BlockSpec((tm, rec_w), lambda i: (i, 0)),
                  pl.BlockSpec((tm, rec_w), lambda i: (i, 0)),
                  pl.BlockSpec((tm, half), lambda i: (i, gc)),
                  pl.BlockSpec((tm, half), lambda i: (i, gc + 1)),
                  pl.BlockSpec((1, att_w + rec_w), lambda i: (0, 0))],
        out_specs=pl.BlockSpec((tm, att_w + rec_w), lambda i: (i, 0)),
        out_shape=jax.ShapeDtypeStruct((t, att_w + rec_w), _BF16),
        compiler_params=_params(1),
        name="mix_post",
    )(att, hf, hb, z, z, group_g.reshape(1, att_w + rec_w))


def _xattn_body(q_ref, kv_ref, o_ref, *, heads, hd):
    xw = heads * hd
    for h in range(heads):
        sl = slice(h * hd, (h + 1) * hd)
        s = lax.dot_general(q_ref[:, sl], kv_ref[:, sl], (((1,), (1,)), ((), ())),
                            preferred_element_type=_F32) * (hd ** -0.5)
        p = jnp.exp(s - jnp.max(s, axis=-1, keepdims=True))
        l = jnp.sum(p, axis=-1, keepdims=True)
        o = jnp.dot(p.astype(_BF16), kv_ref[:, xw + h * hd:xw + (h + 1) * hd],
                    preferred_element_type=_F32)
        o_ref[:, sl] = (o / l).astype(o_ref.dtype)


def _cross_attention(q, kv, batch, seq):
    t, xw = q.shape
    n_mem = kv.shape[0] // batch
    tq = _tile(seq, 1024)
    nq = seq // tq
    return pl.pallas_call(
        functools.partial(_xattn_body, heads=X_HEADS, hd=xw // X_HEADS),
        grid=(batch, nq),
        in_specs=[pl.BlockSpec((tq, xw), lambda b, i: (b * nq + i, 0)),
                  pl.BlockSpec((n_mem, 2 * xw), lambda b, i: (b, 0))],
        out_specs=pl.BlockSpec((tq, xw), lambda b, i: (b * nq + i, 0)),
        out_shape=jax.ShapeDtypeStruct((t, xw), _BF16),
        compiler_params=_params(2),
        name="cross_attention",
    )(q, kv)


def _mlp_up_body(xp_ref, x_ref, xn_ref, wg_ref, wv_ref, cwg_ref, cwv_ref, cbg_ref, cbv_ref,
                 o_ref, xe_ref, ug_ref, uv_ref, *, tm, seq_tiles):
    i = pl.program_id(0)
    hs = BF16_SUBLANES

    @pl.when(pl.program_id(1) == 0)
    def _():
        first = (i % seq_tiles) == 0
        last = (i % seq_tiles) == seq_tiles - 1
        xe_ref[0:hs] = jnp.where(first, jnp.zeros_like(xp_ref[0]), xp_ref[0])
        xe_ref[hs:hs + tm] = x_ref[...]
        xe_ref[hs + tm:hs + tm + hs] = jnp.where(last, jnp.zeros_like(xn_ref[0]), xn_ref[0])

    xe = xe_ref[...]

    def conv(w_ref, u_ref, cw_ref, cb_ref):
        u_ref[...] = jnp.dot(xe, w_ref[...], preferred_element_type=_F32)
        cw = cw_ref[...]
        y = u_ref[hs - 1:hs - 1 + tm] * cw[0:1]
        y = y + u_ref[hs:hs + tm] * cw[1:2]
        y = y + u_ref[hs + 1:hs + 1 + tm] * cw[2:3]
        return y + cb_ref[...]

    gate = conv(wg_ref, ug_ref, cwg_ref, cbg_ref)
    val = conv(wv_ref, uv_ref, cwv_ref, cbv_ref)
    o_ref[...] = (jax.nn.gelu(gate, approximate=True) * val).astype(o_ref.dtype)


def _mlp_up(x, w_up, conv_w, conv_b, seq, tm=512, tn=512):
    t, d = x.shape
    d_ff = w_up.shape[1] // 2
    tm, tn = _tile(seq, tm), _tile(d_ff, tn)
    nj = d_ff // tn
    hs = BF16_SUBLANES
    x3 = x.reshape(t // hs, hs, d)
    nb = t // hs
    cb = conv_b.reshape(1, 2 * d_ff)
    return pl.pallas_call(
        functools.partial(_mlp_up_body, tm=tm, seq_tiles=seq // tm),
        grid=(t // tm, nj),
        in_specs=[pl.BlockSpec((1, hs, d), lambda i, j: (jnp.maximum(i * (tm // hs) - 1, 0), 0, 0)),
                  pl.BlockSpec((tm, d), lambda i, j: (i, 0)),
                  pl.BlockSpec((1, hs, d), lambda i, j: (jnp.minimum((i + 1) * (tm // hs), nb - 1), 0, 0)),
                  pl.BlockSpec((d, tn), lambda i, j: (0, j)),
                  pl.BlockSpec((d, tn), lambda i, j: (0, nj + j)),
                  pl.BlockSpec((3, tn), lambda i, j: (0, j)),
                  pl.BlockSpec((3, tn), lambda i, j: (0, nj + j)),
                  pl.BlockSpec((1, tn), lambda i, j: (0, j)),
                  pl.BlockSpec((1, tn), lambda i, j: (0, nj + j))],
        out_specs=pl.BlockSpec((tm, tn), lambda i, j: (i, j)),
        out_shape=jax.ShapeDtypeStruct((t, d_ff), _BF16),
        scratch_shapes=[pltpu.VMEM((tm + 2 * hs, d), _BF16),
                        pltpu.VMEM((tm + 2 * hs, tn), _F32),
                        pltpu.VMEM((tm + 2 * hs, tn), _F32)],
        compiler_params=_params(2),
        name="mlp_up",
    )(x3, x, x3, w_up, w_up, conv_w, conv_w, cb, cb)


def _rope_tables(seq):
    axis_dim = HEAD_DIM // 2
    rows = seq // GRID_W
    row = jnp.repeat(jnp.arange(rows, dtype=_F32), GRID_W)
    col = jnp.tile(jnp.arange(GRID_W, dtype=_F32), rows)
    inv = ROPE_THETA ** (-jnp.arange(0, axis_dim, 2, dtype=_F32) / axis_dim)
    ar, ac = row[:, None] * inv, col[:, None] * inv
    cos_t = jnp.concatenate([jnp.cos(ar), jnp.cos(ar), jnp.cos(ac), jnp.cos(ac)], axis=-1)
    sin_t = jnp.concatenate([-jnp.sin(ar), jnp.sin(ar), -jnp.sin(ac), jnp.sin(ac)], axis=-1)
    return cos_t, sin_t


@jax.jit
def kernel(x, mem, mem_norm, norm_mix, w_in, q_norm, k_norm, rg_conv_w, rg_conv_b, rg_gate_w, rg_gate_b, rg_lambda, group_norm, w_out, norm_cross, w_cq, w_ckv, w_co, norm_ffn, w_up, ffn_conv_w, ffn_conv_b, w_down, final_norm):
    batch, seq, d = x.shape
    depth = w_in.shape[0]
    t = batch * seq
    att_w = d // 2
    kv_w = att_w // GQA_GROUP
    rec_w = d - att_w
    rec_col0 = att_w + 2 * kv_w
    gate_col0 = rec_col0 + rec_w

    cos_t, sin_t = _rope_tables(seq)
    mem_n = _rmsnorm(mem.reshape(-1, d), mem_norm, _BF16)
    h = x.reshape(t, d)
    hn = _rmsnorm(h, norm_mix[0], _BF16)
    for l in range(depth):
        z = _matmul(hn, w_in[l].astype(_BF16))
        q, k, v = _qkv_prep(z, cos_t, sin_t, q_norm[l], k_norm[l], att_w, kv_w, seq)
        att = _attention(q, k, v, batch, seq)
        xc = _rec_conv(z, rg_conv_w[l], rg_conv_b[l], rec_w, rec_col0, seq)
        hf = _rglru_scan(xc, rg_gate_w[l, 0], rg_gate_b[l, 0], rg_lambda[l, 0], batch, seq, False)
        hb = _rglru_scan(xc, rg_gate_w[l, 1], rg_gate_b[l, 1], rg_lambda[l, 1], batch, seq, True)
        y = _mix_post(att, hf, hb, z, group_norm[l], gate_col0)
        h = _matmul(y, w_out[l].astype(_BF16), res=h)
        hn = _rmsnorm(h, norm_cross[l], _BF16)
        qx = _matmul(hn, w_cq[l].astype(_BF16), out_dtype=_BF16)
        kvx = _matmul(mem_n, w_ckv[l].astype(_BF16), out_dtype=_BF16)
        ox = _cross_attention(qx, kvx, batch, seq)
        h = _matmul(ox, w_co[l].astype(_BF16), res=h)
        hn = _rmsnorm(h, norm_ffn[l], _BF16)
        act = _mlp_up(hn, w_up[l].astype(_BF16), ffn_conv_w[l], ffn_conv_b[l], seq)
        h = _matmul(act, w_down[l].astype(_BF16), res=h, tm=1024, tk=2048)
        if l + 1 < depth:
            hn = _rmsnorm(h, norm_mix[l + 1], _BF16)
    return _rmsnorm(h, final_norm, _F32).reshape(batch, seq, d)
```
